```python
import jax
import jax.numpy as jnp
from jax import lax
import numpy as np

D_MODEL = 1024
BATCH = 4
SEQ = 4096
DEPTH = 2
DEC_BATCH = 32
DEC_SEQ = 8
PAST_LEN = 16384
PAGE_SIZE = 128

MIX = D_MODEL
N_MIXERS = 4
GROUP_W = MIX // N_MIXERS
HEAD_DIM = 64
HEADS = GROUP_W // HEAD_DIM
CHUNK = 128
ROPE_BASE = 10000.0
FORGET_BIAS = 3.0
MOE_GROUPS = 4
EXPERTS_PER_GROUP = 4
N_EXPERTS = MOE_GROUPS * EXPERTS_PER_GROUP
TOP_K_IN_GROUP = 2
D_EXPERT = D_MODEL // 4
ALPHA = (2 * DEPTH) ** 0.25
BETA = (8 * DEPTH) ** -0.25
LN_EPS = 1e-5
RMS_EPS = 1e-6
COL_NAMES = ('a_u', 'a_v', 'b_q', 'b_k', 'b_v', 'b_g', 'c_q', 'c_k', 'c_v', 'c_o', 'c_i', 'c_f', 'd_q', 'd_k', 'd_v', 'd_f')
COL_WIDTHS = (GROUP_W,) * 10 + (HEADS, HEADS) + (GROUP_W,) * 3 + (HEADS,)
D_IN = sum(COL_WIDTHS)
VALUE_COLS = ('b_v', 'c_v', 'd_v')

kernel_name = 'hymba_style_hybrid_decoder_step'


def col_offset(name):
    off = 0
    for n, w in zip(COL_NAMES, COL_WIDTHS):
        if n == name:
            return off
        off += w
    raise KeyError(name)


def split_cols(z):
    out, off = {}, 0
    for n, w in zip(COL_NAMES, COL_WIDTHS):
        out[n] = z[..., off:off + w]
        off += w
    return out


def layer_norm(x, g, b):
    xf = x.astype(jnp.float32)
    mu = jnp.mean(xf, -1, keepdims=True)
    var = jnp.mean(jnp.square(xf - mu), -1, keepdims=True)
    y = (xf - mu) * lax.rsqrt(var + LN_EPS) * g.astype(jnp.float32) + b.astype(jnp.float32)
    return y.astype(x.dtype)


def head_norm(y):
    mu = jnp.mean(y, -1, keepdims=True)
    var = jnp.mean(jnp.square(y - mu), -1, keepdims=True)
    return (y - mu) * lax.rsqrt(var + LN_EPS)


def head_rms(y):
    return y * lax.rsqrt(jnp.mean(jnp.square(y), -1, keepdims=True) + RMS_EPS)


def rope(x, pos):
    half = HEAD_DIM // 2
    inv = ROPE_BASE ** (-jnp.arange(half, dtype=jnp.float32) / half)
    ang = pos.astype(jnp.float32)[:, None] * inv[None, :]
    cos = jnp.cos(ang)[None, :, None, :]
    sin = jnp.sin(ang)[None, :, None, :]
    x1, x2 = x[..., :half], x[..., half:]
    return jnp.concatenate([x1 * cos - x2 * sin, x1 * sin + x2 * cos], -1)


def to_chunks(a, L):
    bn, t = a.shape[:2]
    a = a.reshape((bn, t // L, L) + a.shape[2:])
    return jnp.swapaxes(jnp.moveaxis(a, 1, 0), 2, 3)


def from_chunks(a):
    nc, bn, h, L, d = a.shape
    return a.transpose(1, 0, 3, 2, 4).reshape(bn, nc * L, h, d)


def chunk_mlp(u, v, w_s, b_s):
    bn, t = v.shape[:2]
    L = min(CHUNK, t)
    w = jnp.where(jnp.tril(jnp.ones((L, L), dtype=bool)), w_s[:, :L, :L], 0.0)
    vc = v.reshape(bn, t // L, L, HEADS, HEAD_DIM)
    s = jnp.einsum('hts,bcshd->bcthd', w, vc) + b_s[:, :L].T[None, None, :, :, None]
    return u * s.reshape(bn, t, HEADS, HEAD_DIM)


def retention(q, k, v, r0):
    bn, t = q.shape[:2]
    L = min(CHUNK, t)
    log_g = jnp.log1p(-jnp.exp2(-5.0 - jnp.arange(HEADS, dtype=jnp.float32)))
    idx = jnp.arange(L, dtype=jnp.float32)
    diff = idx[:, None] - idx[None, :]
    decay = jnp.where(diff >= 0, jnp.exp(log_g[:, None, None] * jnp.maximum(diff, 0.0)), 0.0)
    q_dec = jnp.exp(log_g[:, None] * (idx + 1.0))[None, :, :, None]
    k_dec = jnp.exp(log_g[:, None] * (L - 1.0 - idx))[None, :, :, None]
    c_dec = jnp.exp(log_g * L)[None, :, None, None]

    def step(r, inp):
        qc, kc, vc = inp
        s = jnp.einsum('bhld,bhmd->bhlm', qc, kc) * decay
        o = jnp.einsum('bhlm,bhmd->bhld', s, vc) + jnp.einsum('bhld,bhde->bhle', qc, r) * q_dec
        r = c_dec * r + jnp.einsum('bhld,bhle->bhde', kc * k_dec, vc)
        return r, o

    r, o = lax.scan(step, r0, (to_chunks(q, L), to_chunks(k, L), to_chunks(v, L)))
    return from_chunks(o), r


def mlstm(q, k, v, i_pre, f_pre, c0, n0, m0):
    bn, t = q.shape[:2]
    L = min(CHUNK, t)
    causal = jnp.tril(jnp.ones((L, L), dtype=bool))

    def step(carry, inp):
        c, n, m = carry
        qc, kc, vc, ic, lfc = inp
        b = jnp.cumsum(lfc, -1)
        a = b + m[..., None]
        dmat = jnp.where(causal, b[..., :, None] - b[..., None, :] + ic[..., None, :], -jnp.inf)
        m_t = jnp.maximum(a, jnp.max(dmat, -1))
        w_in = jnp.exp(dmat - m_t[..., None])
        w_st = jnp.exp(a - m_t)
        s = jnp.einsum('bhld,bhmd->bhlm', qc, kc) * w_in
        num = jnp.einsum('bhlm,bhmd->bhld', s, vc) + w_st[..., None] * jnp.einsum('bhld,bhde->bhle', qc, c)
        den = jnp.sum(s, -1) + w_st * jnp.einsum('bhld,bhd->bhl', qc, n)
        h = num / jnp.maximum(jnp.abs(den), jnp.exp(-m_t))[..., None]
        m_new = m_t[..., -1]
        g_st = jnp.exp(b[..., -1] + m - m_new)
        g_in = jnp.exp(b[..., -1:] - b + ic - m_new[..., None])
        c = g_st[..., None, None] * c + jnp.einsum('bhld,bhle->bhde', kc * g_in[..., None], vc)
        n = g_st[..., None] * n + jnp.einsum('bhl,bhld->bhd', g_in, kc)
        return (c, n, m_new), h

    xs = (to_chunks(q, L), to_chunks(k, L), to_chunks(v, L), to_chunks(i_pre, L),
          to_chunks(jax.nn.log_sigmoid(f_pre), L))
    (c, n, m), h = lax.scan(step, (c0, n0, m0), xs)
    return from_chunks(h), c, n, m


def fox_prompt(q, k, v, logf):
    bn, t = q.shape[:2]
    L = min(CHUNK, t)
    nb = t // L
    scale = HEAD_DIM ** -0.5
    cum = jnp.cumsum(logf, axis=1).transpose(0, 2, 1)
    kpos = jnp.arange(t)

    def block(args):
        qb, cq, start = args
        logits = jnp.einsum('bqhd,bkhd->bhqk', qb, k) * scale + cq[..., :, None] - cum[:, :, None, :]
        qpos = start + jnp.arange(L)
        logits = jnp.where(kpos[None, :] <= qpos[:, None], logits, -jnp.inf)
        return jnp.einsum('bhqk,bkhd->bqhd', jax.nn.softmax(logits, -1), v)

    qbs = jnp.moveaxis(q.reshape(bn, nb, L, HEADS, HEAD_DIM), 1, 0)
    cqs = jnp.moveaxis(cum.reshape(bn, HEADS, nb, L), 2, 0)
    out = lax.map(block, (qbs, cqs, jnp.arange(nb) * L))
    return jnp.moveaxis(out, 0, 1).reshape(bn, t, HEADS, HEAD_DIM)


def fox_sample(q, k, v, logf, k_past, v_past, lf_past):
    f32 = jnp.float32
    t = q.shape[1]
    scale = HEAD_DIM ** -0.5
    k_past, v_past, lf_past = k_past.astype(f32), v_past.astype(f32), lf_past.astype(f32)
    suffix = (lax.cumsum(lf_past, axis=1, reverse=True) - lf_past).transpose(0, 2, 1)
    cn = jnp.cumsum(logf, axis=1).transpose(0, 2, 1)
    lp = jnp.einsum('bqhd,bkhd->bhqk', q, k_past) * scale + cn[..., :, None] + suffix[..., None, :]
    ln = jnp.einsum('bqhd,bkhd->bhqk', q, k) * scale + cn[..., :, None] - cn[..., None, :]
    ln = jnp.where(jnp.tril(jnp.ones((t, t), dtype=bool)), ln, -jnp.inf)
    p = jax.nn.softmax(jnp.concatenate([lp, ln], -1), -1)
    n_past = k_past.shape[1]
    return (jnp.einsum('bhqk,bkhd->bqhd', p[..., :n_past], v_past)
            + jnp.einsum('bhqk,bkhd->bqhd', p[..., n_past:], v))


def gather_pages(pool, page_table):
    g = pool[page_table]
    return g.reshape((page_table.shape[0], page_table.shape[1] * PAGE_SIZE) + pool.shape[2:])


def hier_moe(x, w_rg, b_rg, w_re, b_re, w1, w3, w2):
    f32 = jnp.float32
    xf = x.astype(f32)
    g_logits = xf @ w_rg.astype(f32) + b_rg.astype(f32)
    g_idx = jnp.argmax(g_logits, -1)
    g_w = jnp.max(jax.nn.softmax(g_logits, -1), -1, keepdims=True)
    e_all = jnp.einsum('btd,gde->btge', xf, w_re.astype(f32)) + b_re.astype(f32)
    e_logits = jnp.einsum('btge,btg->bte', e_all, jax.nn.one_hot(g_idx, MOE_GROUPS, dtype=f32))
    top_v, top_i = lax.top_k(e_logits, TOP_K_IN_GROUP)
    w_pair = g_w * jax.nn.softmax(top_v, -1)
    eid = g_idx[..., None] * EXPERTS_PER_GROUP + top_i
    gates = jnp.sum(jax.nn.one_hot(eid, N_EXPERTS, dtype=f32) * w_pair[..., None], -2)
    h = jax.nn.silu(jnp.einsum('btd,edf->btef', x, w1)) * jnp.einsum('btd,edf->btef', x, w3)
    return jnp.einsum('btef,efd->btd', h * gates[..., None].astype(x.dtype), w2)


def mixing_block(x, pos, w_in, b_in, w_s, b_s, g_mix, w_out, r0, c0, n0, m0, past):
    f32 = jnp.float32
    bn, t = x.shape[:2]
    z = split_cols((x @ w_in + b_in).astype(f32))
    hd = lambda a: a.reshape(bn, t, HEADS, HEAD_DIM)
    scale = HEAD_DIM ** -0.5
    u_a = jax.nn.gelu(hd(z['a_u']))
    v_a = head_norm(jax.nn.gelu(hd(z['a_v'])))
    y_a = chunk_mlp(u_a, v_a, w_s.astype(f32), b_s.astype(f32))
    y_b, r_new = retention(rope(hd(z['b_q']), pos), rope(hd(z['b_k']), pos) * scale, hd(z['b_v']), r0.astype(f32))
    y_c, c_new, n_new, m_new = mlstm(hd(z['c_q']), hd(z['c_k']) * scale, hd(z['c_v']), z['c_i'], z['c_f'],
                                     c0.astype(f32), n0.astype(f32), m0.astype(f32))
    y_c = jax.nn.sigmoid(hd(z['c_o'])) * y_c
    q_d, k_d, v_d = hd(z['d_q']), hd(z['d_k']), hd(z['d_v'])
    logf = jax.nn.log_sigmoid(z['d_f'])
    if past is None:
        y_d = fox_prompt(q_d, k_d, v_d, logf)
    else:
        y_d = fox_sample(q_d, k_d, v_d, logf, past[0], past[1], past[2])
    merged = jnp.concatenate([head_rms(y_a), head_rms(y_b) * jax.nn.silu(hd(z['b_g'])),
                              head_rms(y_c), head_rms(y_d)], axis=2)
    merged = merged.reshape(bn, t, MIX) * g_mix.astype(f32)
    out = merged.astype(x.dtype) @ w_out
    return out, (v_a, r_new, c_new, n_new, m_new, k_d, v_d, logf)


def stack_layers(states, i, dtype):
    return jnp.stack([s[i] for s in states]).astype(dtype)


def setup_inputs(seed: int = 0) -> dict:
    key = jax.random.key(seed)
    ks = jax.random.split(key, 32)
    f32 = jnp.float32
    nrm = lambda k, shape, s: s * jax.random.normal(k, shape, f32)
    n_pages = PAST_LEN // PAGE_SIZE
    n_used = DEC_BATCH * n_pages
    n_pool = (5 * n_used) // 4
    col_scale = jnp.concatenate([jnp.full((w,), BETA if n in VALUE_COLS else 1.0, f32)
                                 for n, w in zip(COL_NAMES, COL_WIDTHS)])
    b_in = nrm(ks[11], (DEPTH, D_IN), 0.02)
    for n in ('c_f', 'd_f'):
        o = col_offset(n)
        b_in = b_in.at[:, o:o + HEADS].add(FORGET_BIAS)
    cache_logf = -jax.random.uniform(ks[4], (DEPTH, n_pool, PAGE_SIZE, HEADS), f32, 0.0, 2.0 / PAST_LEN)
    return {
        'x_prompt': nrm(ks[0], (BATCH, SEQ, D_MODEL), 1.0),
        'x_sample': nrm(ks[1], (DEC_BATCH, DEC_SEQ, D_MODEL), 1.0),
        'cache_k': nrm(ks[2], (DEPTH, n_pool, PAGE_SIZE, HEADS, HEAD_DIM), 1.0),
        'cache_v': nrm(ks[3], (DEPTH, n_pool, PAGE_SIZE, HEADS, HEAD_DIM), 1.0),
        'cache_logf': cache_logf,
        'page_table': jax.random.permutation(ks[5], n_pool)[:n_used].reshape(DEC_BATCH, n_pages).astype(jnp.int32),
        'state_ret': nrm(ks[6], (DEPTH, DEC_BATCH, HEADS, HEAD_DIM, HEAD_DIM), 1.0),
        'state_mlstm_c': nrm(ks[7], (DEPTH, DEC_BATCH, HEADS, HEAD_DIM, HEAD_DIM), 1.0),
        'state_mlstm_n': nrm(ks[8], (DEPTH, DEC_BATCH, HEADS, HEAD_DIM), 1.0),
        'state_mlstm_m': nrm(ks[9], (DEPTH, DEC_BATCH, HEADS), 1.0),
        'w_in': nrm(ks[10], (DEPTH, D_MODEL, D_IN), D_MODEL ** -0.5) * col_scale,
        'b_in': b_in,
        'w_s': nrm(ks[12], (DEPTH, HEADS, CHUNK, CHUNK), CHUNK ** -0.5),
        'b_s': 1.0 + nrm(ks[13], (DEPTH, HEADS, CHUNK), 0.02),
        'g_mix': 1.0 + nrm(ks[14], (DEPTH, MIX), 0.02),
        'w_out': nrm(ks[15], (DEPTH, MIX, D_MODEL), BETA * MIX ** -0.5),
        'ln1_g': 1.0 + nrm(ks[16], (DEPTH, D_MODEL), 0.02),
        'ln1_b': nrm(ks[17], (DEPTH, D_MODEL), 0.02),
        'w_rg': nrm(ks[18], (DEPTH, D_MODEL, MOE_GROUPS), D_MODEL ** -0.5),
        'b_rg': nrm(ks[19], (DEPTH, MOE_GROUPS), 0.01),
        'w_re': nrm(ks[20], (DEPTH, MOE_GROUPS, D_MODEL, EXPERTS_PER_GROUP), D_MODEL ** -0.5),
        'b_re': nrm(ks[21], (DEPTH, MOE_GROUPS, EXPERTS_PER_GROUP), 0.01),
        'w1': nrm(ks[22], (DEPTH, N_EXPERTS, D_MODEL, D_EXPERT), D_MODEL ** -0.5),
        'w3': nrm(ks[23], (DEPTH, N_EXPERTS, D_MODEL, D_EXPERT), D_MODEL ** -0.5),
        'w2': nrm(ks[24], (DEPTH, N_EXPERTS, D_EXPERT, D_MODEL), BETA * D_EXPERT ** -0.5),
        'ln2_g': 1.0 + nrm(ks[25], (DEPTH, D_MODEL), 0.02),
        'ln2_b': nrm(ks[26], (DEPTH, D_MODEL), 0.02),
    }


def reference(x_prompt, x_sample, cache_k, cache_v, cache_logf, page_table,
              state_ret, state_mlstm_c, state_mlstm_n, state_mlstm_m,
              w_in, b_in, w_s, b_s, g_mix, w_out, ln1_g, ln1_b,
              w_rg, b_rg, w_re, b_re, w1, w3, w2, ln2_g, ln2_b):
    f32 = jnp.float32
    dt = x_prompt.dtype
    bp, tp = x_prompt.shape[:2]
    ts = x_sample.shape[1]
    past_len = page_table.shape[1] * PAGE_SIZE
    pos_p = jnp.arange(tp, dtype=jnp.int32)
    pos_s = past_len + jnp.arange(ts, dtype=jnp.int32)
    hp, hs = x_prompt, x_sample
    st_p, st_s = [], []
    for l in range(DEPTH):
        mix_w = (w_in[l], b_in[l], w_s[l], b_s[l], g_mix[l], w_out[l])
        moe_w = (w_rg[l], b_rg[l], w_re[l], b_re[l], w1[l], w3[l], w2[l])
        zc = jnp.zeros((bp, HEADS, HEAD_DIM, HEAD_DIM), f32)
        mix, st = mixing_block(hp, pos_p, *mix_w, zc, zc, jnp.zeros((bp, HEADS, HEAD_DIM), f32),
                               jnp.zeros((bp, HEADS), f32), None)
        hp = layer_norm(ALPHA * hp + mix, ln1_g[l], ln1_b[l])
        hp = layer_norm(ALPHA * hp + hier_moe(hp, *moe_w), ln2_g[l], ln2_b[l])
        st_p.append(st)
        past = (gather_pages(cache_k[l], page_table), gather_pages(cache_v[l], page_table),
                gather_pages(cache_logf[l], page_table))
        mix, st = mixing_block(hs, pos_s, *mix_w, state_ret[l], state_mlstm_c[l], state_mlstm_n[l],
                               state_mlstm_m[l], past)
        hs = layer_norm(ALPHA * hs + mix, ln1_g[l], ln1_b[l])
        hs = layer_norm(ALPHA * hs + hier_moe(hs, *moe_w), ln2_g[l], ln2_b[l])
        st_s.append(st)
    return (hp, hs,
            stack_layers(st_p, 5, dt), stack_layers(st_p, 6, dt), stack_layers(st_p, 7, dt),
            stack_layers(st_p, 1, dt), stack_layers(st_p, 2, dt), stack_layers(st_p, 3, dt), stack_layers(st_p, 4, dt),
            stack_layers(st_s, 5, dt), stack_layers(st_s, 6, dt), stack_layers(st_s, 7, dt),
            stack_layers(st_s, 1, dt), stack_layers(st_s, 2, dt), stack_layers(st_s, 3, dt), stack_layers(st_s, 4, dt),
            stack_layers(st_s, 0, dt))
```

```python
import functools
import math

import jax
import jax.numpy as jnp
import numpy as np
from jax import lax
from jax.experimental import pallas as pl
from jax.experimental.pallas import tpu as pltpu

f32 = jnp.float32
bf16 = jnp.bfloat16

D_MODEL = 1024
HEADS = 4
HEAD_DIM = 64
GROUP_W = HEADS * HEAD_DIM
CHUNK = 128
PAGE_SIZE = 128
ROPE_BASE = 10000.0
MOE_GROUPS = 4
EXPERTS_PER_GROUP = 4
N_EXPERTS = MOE_GROUPS * EXPERTS_PER_GROUP
D_EXPERT = D_MODEL // 4
LN_EPS = 1e-5
RMS_EPS = 1e-6
TILE = 128
SW = HEADS * TILE
N_ZBLK = 16
Z_W = N_ZBLK * GROUP_W
VMEM_LIMIT = 56 * 1024 * 1024

ZB = dict(a_u=0, a_v=1, b_q=2, b_qs=3, b_k=4, b_ks=5, b_v=6, b_g=7,
          c_q=8, c_k=9, c_v=10, c_o=11, d_q=12, d_k=13, d_v=14, gates=15)
GL_DF, GL_CI, GL_CF = 0, 4, 8

COL_NAMES = ('a_u', 'a_v', 'b_q', 'b_k', 'b_v', 'b_g', 'c_q', 'c_k', 'c_v', 'c_o', 'c_i', 'c_f', 'd_q', 'd_k', 'd_v', 'd_f')
COL_WIDTHS = (GROUP_W,) * 10 + (HEADS, HEADS) + (GROUP_W,) * 3 + (HEADS,)


def _col_offset(name):
    off = 0
    for n, w in zip(COL_NAMES, COL_WIDTHS):
        if n == name:
            return off
        off += w
    raise KeyError(name)


def _dot(a, b):
    return jnp.dot(a, b, preferred_element_type=f32)


def _dot_nt(a, b):
    return lax.dot_general(a, b, (((1,), (1,)), ((), ())), preferred_element_type=f32)


def _dot_tn(a, b):
    return lax.dot_general(a, b, (((0,), (0,)), ((), ())), preferred_element_type=f32)


def _split3(x):
    h = x.astype(bf16)
    r = x - h.astype(f32)
    m = r.astype(bf16)
    lo = (r - m.astype(f32)).astype(bf16)
    return h, m, lo


def _pdot_l(x, sel):
    h, m, lo = _split3(x)
    return _dot(h, sel) + _dot(m, sel) + _dot(lo, sel)


def _pdot_r(sel, x):
    h, m, lo = _split3(x)
    return _dot(sel, h) + _dot(sel, m) + _dot(sel, lo)


def _hsum(x, bd):
    h = x.astype(bf16)
    m = (x - h.astype(f32)).astype(bf16)
    return _dot(h, bd) + _dot(m, bd)


def _head_rms(y, bd):
    return y * lax.rsqrt(_hsum(y * y, bd) * (1.0 / HEAD_DIM) + RMS_EPS)


def _log_sigmoid(x):
    return jnp.minimum(x, 0.0) - jnp.log1p(jnp.exp(-jnp.abs(x)))


def _lane_head(shape):
    return lax.shift_right_logical(lax.broadcasted_iota(jnp.int32, shape, 1), 6)


def _headmap(cols, hid):
    out = jnp.broadcast_to(cols[HEADS - 1], hid.shape)
    for h in range(HEADS - 2, -1, -1):
        out = jnp.where(hid == h, jnp.broadcast_to(cols[h], hid.shape), out)
    return out


def _stack_heads(xb, hid):
    zero = jnp.zeros_like(xb)
    return jnp.concatenate([jnp.where(hid == h, xb, zero) for h in range(HEADS)], axis=0)


def _row_seg(shape, seg_len):
    return lax.broadcasted_iota(jnp.int32, shape, 0) // seg_len


def _inproj_kernel(x_ref, w_ref, b_ref, o_ref):
    xb = x_ref[...].astype(bf16)
    for c in range(N_ZBLK // 4):
        sl = slice(c * 4 * GROUP_W, (c + 1) * 4 * GROUP_W)
        o_ref[:, sl] = _dot(xb, w_ref[:, sl]) + b_ref[:, sl]


def _inproj(x2d, w_big, b_big):
    n = x2d.shape[0]
    tm = min(512, n)
    return pl.pallas_call(
        _inproj_kernel,
        grid=(n // tm,),
        in_specs=[pl.BlockSpec((tm, D_MODEL), lambda i: (i, 0)),
                  pl.BlockSpec((D_MODEL, Z_W), lambda i: (0, 0)),
                  pl.BlockSpec((1, Z_W), lambda i: (0, 0))],
        out_specs=pl.BlockSpec((tm, Z_W), lambda i: (i, 0)),
        out_shape=jax.ShapeDtypeStruct((n, Z_W), f32),
        compiler_params=pltpu.CompilerParams(dimension_semantics=("arbitrary",), vmem_limit_bytes=VMEM_LIMIT),
        name="inproj",
    )(x2d, w_big, b_big)


def _zspec(name):
    j = ZB[name]
    return pl.BlockSpec((None, TILE, GROUP_W), lambda b, c: (b, c, j))


def _const_spec(shape):
    nd = len(shape)
    return pl.BlockSpec(shape, lambda b, c: (0,) * nd)


def _mixa_kernel(zu_ref, zv_ref, w_ref, bmap_ref, gm_ref, bd_ref, y_ref, vn_ref):
    bd = bd_ref[...]
    u = jax.nn.gelu(zu_ref[...])
    v = jax.nn.gelu(zv_ref[...])
    mu = _hsum(v, bd) * (1.0 / HEAD_DIM)
    vc = v - mu
    var = _hsum(vc * vc, bd) * (1.0 / HEAD_DIM)
    vn = vc * lax.rsqrt(var + LN_EPS)
    vn_ref[...] = vn
    vb = vn.astype(bf16)
    hid = _lane_head(vn.shape)
    s = _dot(w_ref[HEADS - 1], vb)
    for h in range(HEADS - 2, -1, -1):
        s = jnp.where(hid == h, _dot(w_ref[h], vb), s)
    y = u * (s + bmap_ref[...])
    y_ref[...] = (_head_rms(y, bd) * gm_ref[...]).astype(bf16)


def _mix_a(z, w_blk, bmap, gm, bd):
    b, t = z.shape[:2]
    return pl.pallas_call(
        _mixa_kernel,
        grid=(b, t // TILE),
        in_specs=[_zspec('a_u'), _zspec('a_v'), _const_spec((HEADS, TILE, TILE)), _const_spec((TILE, GROUP_W)),
                  _const_spec((1, GROUP_W)), _const_spec((GROUP_W, GROUP_W))],
        out_specs=[pl.BlockSpec((None, TILE, GROUP_W), lambda b, c: (b, c, 0)),
                   pl.BlockSpec((None, TILE, GROUP_W), lambda b, c: (b, c, 0))],
        out_shape=[jax.ShapeDtypeStruct((b, t, GROUP_W), bf16), jax.ShapeDtypeStruct((b, t, GROUP_W), f32)],
        compiler_params=pltpu.CompilerParams(dimension_semantics=("arbitrary", "arbitrary"), vmem_limit_bytes=VMEM_LIMIT),
        name="mix_a",
    )(z, z, w_blk, bmap, gm, bd)


def _mixb_kernel(*refs, nseg, has_init):
    (q_ref, qs_ref, k_ref, ks_ref, v_ref, g_ref, cos_ref, sin_ref, dec_ref, qdec_ref, kdec_ref,
     cd_ref, bdf_ref, gm_ref, bd_ref) = refs[:15]
    refs = refs[15:]
    if has_init:
        r0_ref, refs = refs[0], refs[1:]
    y_ref, rout_ref, r_scr = refs
    c = pl.program_id(1)
    seg_len = TILE // nseg

    @pl.when(c == 0)
    def _():
        if has_init:
            r_scr[...] = r0_ref[...]
        else:
            r_scr[...] = jnp.zeros_like(r_scr)

    bd = bd_ref[...]
    cos = cos_ref[...]
    sin = sin_ref[...]
    qr = q_ref[...] * cos + qs_ref[...] * sin
    kr = k_ref[...] * cos + ks_ref[...] * sin
    v = v_ref[...]
    hid = _lane_head(v.shape)
    qb = qr.astype(bf16)
    vb = v.astype(bf16)
    kst = _stack_heads(kr.astype(bf16), hid)
    vst = _stack_heads(vb, hid)
    s_all = _dot_nt(qb, kst) * dec_ref[...]
    o = _dot(s_all.astype(bf16), vst)
    kd = (kr * kdec_ref[...]).astype(bf16)
    cd = cd_ref[...]
    bdf = bdf_ref[...]
    if nseg == 1:
        r_old = r_scr[0]
        cross = _dot(qb, r_old.astype(bf16))
        r_scr[0] = r_old * cd + _dot_tn(kd, vb) * bdf
    else:
        rseg = _row_seg(v.shape, seg_len)
        cross = jnp.zeros_like(v)
        zero = jnp.zeros_like(kd)
        for i in range(nseg):
            r_old = r_scr[i]
            cross = jnp.where(rseg == i, _dot(qb, r_old.astype(bf16)), cross)
            r_scr[i] = r_old * cd + _dot_tn(jnp.where(rseg == i, kd, zero), vb) * bdf
    y = o + cross * qdec_ref[...]
    g = g_ref[...]
    yn = _head_rms(y, bd) * (g * jax.nn.sigmoid(g))
    y_ref[...] = (yn * gm_ref[...]).astype(bf16)

    @pl.when(c == pl.num_programs(1) - 1)
    def _():
        rout_ref[...] = r_scr[...]


def _mix_b(z, cos, sin, consts, gm, bd, r0, nseg):
    b, t = z.shape[:2]
    has_init = r0 is not None
    dec, qdec, kdec, cd, bdf = consts
    tab_spec = pl.BlockSpec((TILE, GROUP_W), lambda b, c: (c, 0))
    st_spec = pl.BlockSpec((nseg, GROUP_W, GROUP_W), lambda b, c: (b, 0, 0))
    in_specs = [_zspec('b_q'), _zspec('b_qs'), _zspec('b_k'), _zspec('b_ks'), _zspec('b_v'), _zspec('b_g'),
                tab_spec, tab_spec, _const_spec((TILE, SW)), _const_spec((TILE, GROUP_W)), _const_spec((TILE, GROUP_W)),
                _const_spec((GROUP_W, GROUP_W)), _const_spec((GROUP_W, GROUP_W)), _const_spec((1, GROUP_W)),
                _const_spec((GROUP_W, GROUP_W))]
    args = [z, z, z, z, z, z, cos, sin, dec, qdec, kdec, cd, bdf, gm, bd]
    if has_init:
        in_specs.append(st_spec)
        args.append(r0)
    return pl.pallas_call(
        functools.partial(_mixb_kernel, nseg=nseg, has_init=has_init),
        grid=(b, t // TILE),
        in_specs=in_specs,
        out_specs=[pl.BlockSpec((None, TILE, GROUP_W), lambda b, c: (b, c, 0)), st_spec],
        out_shape=[jax.ShapeDtypeStruct((b, t, GROUP_W), bf16),
                   jax.ShapeDtypeStruct((b * nseg, GROUP_W, GROUP_W), f32)],
        scratch_shapes=[pltpu.VMEM((nseg, GROUP_W, GROUP_W), f32)],
        compiler_params=pltpu.CompilerParams(dimension_semantics=("arbitrary", "arbitrary"), vmem_limit_bytes=VMEM_LIMIT),
        name="mix_b",
    )(*args)


def _mixc_kernel(*refs, nseg, has_init):
    (q_ref, k_ref, v_ref, og_ref, gt_ref, mask_ref, tri_ref, last_ref, seli_ref, self_ref,
     bdf_ref, gm_ref, bd_ref) = refs[:13]
    refs = refs[13:]
    if has_init:
        c0_ref, n0_ref, m0_ref = refs[:3]
        refs = refs[3:]
    y_ref, cout_ref, nout_ref, mout_ref, c_scr, n_scr, m_scr = refs
    c = pl.program_id(1)
    seg_len = TILE // nseg

    @pl.when(c == 0)
    def _():
        if has_init:
            c_scr[...] = c0_ref[...]
            n_scr[...] = n0_ref[...]
            m_scr[...] = m0_ref[...]
        else:
            c_scr[...] = jnp.zeros_like(c_scr)
            n_scr[...] = jnp.zeros_like(n_scr)
            m_scr[...] = jnp.zeros_like(m_scr)

    bd = bd_ref[...]
    bdf = bdf_ref[...]
    q = q_ref[...]
    k = k_ref[...]
    v = v_ref[...]
    hid = _lane_head(v.shape)
    qb = q.astype(bf16)
    vb = v.astype(bf16)
    kst = _stack_heads(k.astype(bf16), hid)
    vst = _stack_heads(vb, hid)
    qk_all = _dot_nt(qb, kst)

    gts = gt_ref[...]
    lf = _log_sigmoid(gts)
    bfull = _pdot_r(tri_ref[...], lf)
    bc_b = _pdot_l(bfull, self_ref[...])
    u = _pdot_l(gts, seli_ref[...]) - bc_b
    m_prev = m_scr[...]
    a_all = bc_b + m_prev
    maskc = mask_ref[...] > 0.5

    s_blocks, mt_cols, wst_cols, rs_cols = [], [], [], []
    for h in range(HEADS):
        hb = slice(h * TILE, (h + 1) * TILE)
        dm = jnp.where(maskc, bc_b[:, hb] + u[:, hb].T, -jnp.inf)
        a_h = a_all[:, h * TILE:h * TILE + 1]
        mt = jnp.maximum(a_h, jnp.max(dm, axis=-1, keepdims=True))
        s_h = qk_all[:, hb] * jnp.exp(dm - mt)
        s_blocks.append(s_h)
        mt_cols.append(mt)
        wst_cols.append(jnp.exp(a_h - mt))
        rs_cols.append(jnp.sum(s_h, axis=-1, keepdims=True))
    s_all = jnp.concatenate(s_blocks, axis=1)
    num = _dot(s_all.astype(bf16), vst)

    rowhead = lax.shift_right_logical(lax.broadcasted_iota(jnp.int32, (GROUP_W, TILE), 0), 6)

    def nmat(ncol):
        nb = ncol.astype(bf16)
        zero = jnp.zeros_like(nb)
        return jnp.concatenate([jnp.where(rowhead == h, nb, zero) for h in range(HEADS)], axis=1)

    if nseg == 1:
        cross = _dot(qb, c_scr[0].astype(bf16))
        qn = _dot(qb, nmat(n_scr[0]))
    else:
        rseg = _row_seg(v.shape, seg_len)
        rseg_w = _row_seg((TILE, SW), seg_len)
        cross = jnp.zeros_like(v)
        qn = jnp.zeros((TILE, SW), f32)
        for i in range(nseg):
            cross = jnp.where(rseg == i, _dot(qb, c_scr[i].astype(bf16)), cross)
            qn = jnp.where(rseg_w == i, _dot(qb, nmat(n_scr[i])), qn)

    den_cols = []
    for h in range(HEADS):
        den = rs_cols[h] + wst_cols[h] * qn[:, h * TILE:h * TILE + 1]
        den_cols.append(jnp.maximum(jnp.abs(den), jnp.exp(-mt_cols[h])))
    hout = (num + _headmap(wst_cols, hid) * cross) / _headmap(den_cols, hid)
    y = jax.nn.sigmoid(og_ref[...]) * hout
    y_ref[...] = (_head_rms(y, bd) * gm_ref[...]).astype(bf16)

    last = last_ref[...]
    mt_bc = jnp.concatenate([jnp.broadcast_to(mt_cols[h], (TILE, TILE)) for h in range(HEADS)], axis=1)
    m_new = _pdot_r(last, mt_bc)
    b_last = _pdot_r(last, bc_b)
    gst = jnp.exp(b_last + m_prev - m_new)
    gin = jnp.exp(b_last + u - m_new)
    gin_map = _headmap([gin[:, h * TILE:h * TILE + 1] for h in range(HEADS)], hid)
    kg = (k * gin_map).astype(bf16)
    ones = jnp.ones((TILE, TILE), bf16)
    zero = jnp.zeros_like(kg)
    for i in range(nseg):
        r0 = i * seg_len
        kg_i = kg if nseg == 1 else jnp.where(rseg == i, kg, zero)
        gst_rows = jnp.concatenate(
            [jnp.broadcast_to(gst[r0:r0 + 1, h * TILE:(h + 1) * TILE], (HEAD_DIM, TILE)) for h in range(HEADS)], axis=0)
        c_scr[i] = jnp.concatenate([gst_rows, gst_rows], axis=1) * c_scr[i] + _dot_tn(kg_i, vb) * bdf
        n_scr[i] = gst_rows * n_scr[i] + _dot_tn(kg_i, ones)
    m_scr[...] = m_new

    @pl.when(c == pl.num_programs(1) - 1)
    def _():
        cout_ref[...] = c_scr[...]
        nout_ref[...] = n_scr[...]
        mout_ref[...] = m_scr[...]


def _mix_c(z, consts, gm, bd, init, nseg):
    b, t = z.shape[:2]
    has_init = init is not None
    maskc, tri, last, seli, self_, bdf = consts
    c_spec = pl.BlockSpec((nseg, GROUP_W, GROUP_W), lambda b, c: (b, 0, 0))
    n_spec = pl.BlockSpec((nseg, GROUP_W, TILE), lambda b, c: (b, 0, 0))
    m_spec = pl.BlockSpec((None, TILE, SW), lambda b, c: (b, 0, 0))
    in_specs = [_zspec('c_q'), _zspec('c_k'), _zspec('c_v'), _zspec('c_o'), _zspec('gates'),
                _const_spec((TILE, TILE)), _const_spec((TILE, TILE)), _const_spec((TILE, TILE)),
                _const_spec((GROUP_W, SW)), _const_spec((GROUP_W, SW)), _const_spec((GROUP_W, GROUP_W)),
                _const_spec((1, GROUP_W)), _const_spec((GROUP_W, GROUP_W))]
    args = [z, z, z, z, z, maskc, tri, last, seli, self_, bdf, gm, bd]
    if has_init:
        in_specs += [c_spec, n_spec, m_spec]
        args += list(init)
    return pl.pallas_call(
        functools.partial(_mixc_kernel, nseg=nseg, has_init=has_init),
        grid=(b, t // TILE),
        in_specs=in_specs,
        out_specs=[pl.BlockSpec((None, TILE, GROUP_W), lambda b, c: (b, c, 0)), c_spec, n_spec, m_spec],
        out_shape=[jax.ShapeDtypeStruct((b, t, GROUP_W), bf16),
                   jax.ShapeDtypeStruct((b * nseg, GROUP_W, GROUP_W), f32),
                   jax.ShapeDtypeStruct((b * nseg, GROUP_W, TILE), f32),
                   jax.ShapeDtypeStruct((b, TILE, SW), f32)],
        scratch_shapes=[pltpu.VMEM((nseg, GROUP_W, GROUP_W), f32), pltpu.VMEM((nseg, GROUP_W, TILE), f32),
                        pltpu.VMEM((TILE, SW), f32)],
        compiler_params=pltpu.CompilerParams(dimension_semantics=("arbitrary", "arbitrary"), vmem_limit_bytes=VMEM_LIMIT),
        name="mix_c",
    )(*args)


def _foxpre_kernel(gt_ref, tri_ref, lf_ref, col_ref, row_ref, carry_scr, *, carry):
    c = pl.program_id(1)
    lf = _log_sigmoid(gt_ref[...])[:, :TILE]
    cum = _pdot_r(tri_ref[...], lf)
    if carry:
        @pl.when(c == 0)
        def _():
            carry_scr[...] = jnp.zeros_like(carry_scr)

        cum = cum + carry_scr[...]
        carry_scr[...] = jnp.broadcast_to(cum[TILE - 1:TILE, :], cum.shape)
    lf_ref[...] = lf
    col_ref[...] = cum
    ct = cum.T
    row = jnp.concatenate([ct[GL_DF + h:GL_DF + h + 1, :] for h in range(HEADS)], axis=1)
    row_ref[...] = jnp.broadcast_to(row, (8, SW))


def _fox_pre(z, tri, carry):
    b, t = z.shape[:2]
    nc = t // TILE
    return pl.pallas_call(
        functools.partial(_foxpre_kernel, carry=carry),
        grid=(b, nc),
        in_specs=[_zspec('gates'), _const_spec((TILE, TILE))],
        out_specs=[pl.BlockSpec((None, TILE, TILE), lambda b, c: (b, c, 0)),
                   pl.BlockSpec((None, TILE, TILE), lambda b, c: (b, c, 0)),
                   pl.BlockSpec((None, None, 8, SW), lambda b, c: (b, c, 0, 0))],
        out_shape=[jax.ShapeDtypeStruct((b, t, TILE), f32), jax.ShapeDtypeStruct((b, t, TILE), f32),
                   jax.ShapeDtypeStruct((b, nc, 8, SW), f32)],
        scratch_shapes=[pltpu.VMEM((TILE, TILE), f32)],
        compiler_params=pltpu.CompilerParams(dimension_semantics=("arbitrary", "arbitrary"), vmem_limit_bytes=VMEM_LIMIT),
        name="fox_pre",
    )(z, tri)


FOX_TQ = 256


def _fox_kernel(q_ref, k_ref, v_ref, col_ref, row_ref, selc_ref, gm_ref, bd_ref, y_ref, kst_scr, vst_scr):
    qi = pl.program_id(1)
    nsub = FOX_TQ // TILE
    t_len = k_ref.shape[0]

    @pl.when(qi == 0)
    def _():
        def build(j, carry):
            rows = pl.ds(pl.multiple_of(j * TILE, TILE), TILE)
            hid_k = _lane_head((TILE, GROUP_W))
            kst_scr[j] = _stack_heads(k_ref[rows, :].astype(bf16), hid_k)
            vst_scr[j] = _stack_heads(v_ref[rows, :].astype(bf16), hid_k)
            return carry
        lax.fori_loop(0, t_len // TILE, build, 0)

    qb = q_ref[...].astype(bf16)
    hid = _lane_head((FOX_TQ, GROUP_W))
    cq = _pdot_l(col_ref[...], selc_ref[...])

    def step(j, carry, masked):
        m, l, acc = carry
        logits = _dot_nt(qb, kst_scr[j]) + cq - row_ref[j][0:1, :]
        if masked:
            qpos = qi * FOX_TQ + lax.broadcasted_iota(jnp.int32, (FOX_TQ, TILE), 0)
            kpos = j * TILE + lax.broadcasted_iota(jnp.int32, (FOX_TQ, TILE), 1)
            keep = kpos <= qpos
        m_new, l_new, p_blocks, alphas = [], [], [], []
        for h in range(HEADS):
            s_h = logits[:, h * TILE:(h + 1) * TILE]
            if masked:
                s_h = jnp.where(keep, s_h, -jnp.inf)
            mh = jnp.maximum(m[h], jnp.max(s_h, axis=-1, keepdims=True))
            p = jnp.exp(s_h - mh)
            alpha = jnp.exp(m[h] - mh)
            m_new.append(mh)
            l_new.append(alpha * l[h] + jnp.sum(p, axis=-1, keepdims=True))
            p_blocks.append(p.astype(bf16))
            alphas.append(alpha)
        pv = _dot(jnp.concatenate(p_blocks, axis=1), vst_scr[j])
        acc = acc * _headmap(alphas, hid) + pv
        return tuple(m_new), tuple(l_new), acc

    init = (tuple(jnp.full((FOX_TQ, 1), -jnp.inf, f32) for _ in range(HEADS)),
            tuple(jnp.zeros((FOX_TQ, 1), f32) for _ in range(HEADS)),
            jnp.zeros((FOX_TQ, GROUP_W), f32))
    carry = lax.fori_loop(0, qi * nsub, lambda j, cr: step(j, cr, False), init)
    for d in range(nsub):
        carry = step(qi * nsub + d, carry, True)
    _, l, acc = carry
    y = acc / _headmap(list(l), hid)
    y_ref[...] = (_head_rms(y, bd_ref[...]) * gm_ref[...]).astype(bf16)


def _fox_prompt(z, cum_col, cum_row, selc, gm, bd):
    b, t = z.shape[:2]
    nk = t // TILE
    jq, jk, jv = ZB['d_q'], ZB['d_k'], ZB['d_v']
    return pl.pallas_call(
        _fox_kernel,
        grid=(b, t // FOX_TQ),
        in_specs=[pl.BlockSpec((None, FOX_TQ, GROUP_W), lambda b, i: (b, i, jq)),
                  pl.BlockSpec((None, t, GROUP_W), lambda b, i: (b, 0, jk)),
                  pl.BlockSpec((None, t, GROUP_W), lambda b, i: (b, 0, jv)),
                  pl.BlockSpec((None, FOX_TQ, TILE), lambda b, i: (b, i, 0)),
                  pl.BlockSpec((None, nk, 8, SW), lambda b, i: (b, 0, 0, 0)),
                  _const_spec((TILE, SW)), _const_spec((1, GROUP_W)), _const_spec((GROUP_W, GROUP_W))],
        out_specs=pl.BlockSpec((None, FOX_TQ, GROUP_W), lambda b, i: (b, i, 0)),
        out_shape=jax.ShapeDtypeStruct((b, t, GROUP_W), bf16),
        scratch_shapes=[pltpu.VMEM((nk, SW, GROUP_W), bf16), pltpu.VMEM((nk, SW, GROUP_W), bf16)],
        compiler_params=pltpu.CompilerParams(dimension_semantics=("arbitrary", "arbitrary"), vmem_limit_bytes=VMEM_LIMIT),
        name="fox_prompt",
    )(z, z, z, cum_col, cum_row, selc, gm, bd)


PAGES_PER_STEP = 8
LF_PAGES_PER_STEP = 32


def _suffix_kernel(*refs, n_pages):
    pt_ref = refs[0]
    page_refs = refs[1:1 + LF_PAGES_PER_STEP]
    mrev_ref, mtot_ref, trev_ref, out_ref, pages_scr = refs[1 + LF_PAGES_PER_STEP:]
    j = pl.program_id(1)
    for i in range(LF_PAGES_PER_STEP):
        pages_scr[j * (LF_PAGES_PER_STEP // 8) + i // 8, i % 8:i % 8 + 1, :] = page_refs[i][...]

    @pl.when(j == pl.num_programs(1) - 1)
    def _():
        pages = pages_scr[...].reshape(n_pages, PAGE_SIZE * HEADS)
        inpage = _pdot_l(pages, mrev_ref[...])
        tot = _pdot_l(pages, mtot_ref[...])
        out_ref[...] = inpage + _pdot_r(trev_ref[...], tot)


def _fox_suffix(logf_pool, page_table, mrev, mtot, trev):
    nseq, n_pages = page_table.shape
    pool = logf_pool.reshape(logf_pool.shape[0], 1, PAGE_SIZE * HEADS)

    def page_spec(i):
        return pl.BlockSpec((None, 1, PAGE_SIZE * HEADS), lambda s, j, pt: (pt[s, j * LF_PAGES_PER_STEP + i], 0, 0))

    cs = lambda shape: pl.BlockSpec(shape, lambda s, j, pt: (0,) * len(shape))
    grid_spec = pltpu.PrefetchScalarGridSpec(
        num_scalar_prefetch=1,
        grid=(nseq, n_pages // LF_PAGES_PER_STEP),
        in_specs=[page_spec(i) for i in range(LF_PAGES_PER_STEP)]
        + [cs((PAGE_SIZE * HEADS, SW)), cs((PAGE_SIZE * HEADS, SW)), cs((n_pages, n_pages))],
        out_specs=pl.BlockSpec((None, n_pages, SW), lambda s, j, pt: (s, 0, 0)),
        scratch_shapes=[pltpu.VMEM((n_pages // 8, 8, PAGE_SIZE * HEADS), f32)],
    )
    return pl.pallas_call(
        functools.partial(_suffix_kernel, n_pages=n_pages),
        grid_spec=grid_spec,
        out_shape=jax.ShapeDtypeStruct((nseq, n_pages, SW), f32),
        compiler_params=pltpu.CompilerParams(dimension_semantics=("arbitrary", "arbitrary"), vmem_limit_bytes=VMEM_LIMIT),
        name="fox_suffix",
    )(page_table, *([pool] * LF_PAGES_PER_STEP), mrev, mtot, trev)


def _foxs_kernel(*refs, seg_len):
    pt_ref = refs[0]
    k_refs = refs[1:1 + PAGES_PER_STEP]
    v_refs = refs[1 + PAGES_PER_STEP:1 + 2 * PAGES_PER_STEP]
    (q_ref, kn_ref, vn_ref, cn_ref, cnrow_ref, suf_ref, gm_ref, bd_ref,
     y_ref, qst_scr, m_scr, l_scr, acc_scr) = refs[1 + 2 * PAGES_PER_STEP:]
    j = pl.program_id(1)
    rows = HEADS * seg_len
    hid = _lane_head((rows, GROUP_W))
    rowhead = lax.broadcasted_iota(jnp.int32, (rows, GROUP_W), 0) // seg_len

    @pl.when(j == 0)
    def _():
        qb = q_ref[...].astype(bf16)
        qrep = jnp.concatenate([qb] * HEADS, axis=0)
        qst_scr[...] = jnp.where(hid == rowhead, qrep, jnp.zeros_like(qrep))
        m_scr[...] = jnp.full(m_scr.shape, -jnp.inf, f32)
        l_scr[...] = jnp.zeros_like(l_scr)
        acc_scr[...] = jnp.zeros_like(acc_scr)

    qst = qst_scr[...]
    cn = cn_ref[...]

    def update(logits, vals):
        m_old = m_scr[...]
        m_new = jnp.maximum(m_old, jnp.max(logits, axis=-1, keepdims=True))
        p = jnp.exp(logits - m_new)
        alpha = jnp.exp(m_old - m_new)
        l_scr[...] = alpha * l_scr[...] + jnp.sum(p, axis=-1, keepdims=True)
        acc_scr[...] = alpha * acc_scr[...] + _dot(p.astype(bf16), vals)
        m_scr[...] = m_new

    row_h = lax.broadcasted_iota(jnp.int32, (rows, PAGE_SIZE), 0) // seg_len
    for i in range(PAGES_PER_STEP):
        kb = k_refs[i][...].astype(bf16)
        vb = v_refs[i][...].astype(bf16)
        suf = suf_ref[pl.ds(j * PAGES_PER_STEP + i, 1), :]
        bias = jnp.broadcast_to(suf[:, (HEADS - 1) * TILE:], (rows, PAGE_SIZE))
        for h in range(HEADS - 2, -1, -1):
            bias = jnp.where(row_h == h, jnp.broadcast_to(suf[:, h * TILE:(h + 1) * TILE], (rows, PAGE_SIZE)), bias)
        update(_dot_nt(qst, kb) + cn + bias, vb)

    @pl.when(j == pl.num_programs(1) - 1)
    def _():
        pad = jnp.zeros((PAGE_SIZE - seg_len, GROUP_W), f32)
        kb = jnp.concatenate([kn_ref[...], pad], axis=0).astype(bf16)
        vb = jnp.concatenate([vn_ref[...], pad], axis=0).astype(bf16)
        t_row = lax.broadcasted_iota(jnp.int32, (rows, PAGE_SIZE), 0) % seg_len
        s_col = lax.broadcasted_iota(jnp.int32, (rows, PAGE_SIZE), 1)
        logits = _dot_nt(qst, kb) + cn - cnrow_ref[...]
        update(jnp.where(s_col <= t_row, logits, -jnp.inf), vb)
        out = acc_scr[...] / l_scr[...]
        y = out[(HEADS - 1) * seg_len:, :]
        for h in range(HEADS - 2, -1, -1):
            y = jnp.where(hid[:seg_len] == h, out[h * seg_len:(h + 1) * seg_len, :], y)
        y_ref[...] = (_head_rms(y, bd_ref[...]) * gm_ref[...]).astype(bf16)


def _fox_sample(z3, cache_k, cache_v, page_table, cn_bc, cn_row, suffix, gm, bd):
    nseq, seg_len = z3.shape[:2]
    n_pages = page_table.shape[1]
    rows = HEADS * seg_len
    jq, jk, jv = ZB['d_q'], ZB['d_k'], ZB['d_v']

    def page_spec(i):
        return pl.BlockSpec((None, PAGE_SIZE, GROUP_W), lambda s, j, pt: (pt[s, j * PAGES_PER_STEP + i], 0, 0))

    cs = lambda shape: pl.BlockSpec(shape, lambda s, j, pt: (0,) * len(shape))
    grid_spec = pltpu.PrefetchScalarGridSpec(
        num_scalar_prefetch=1,
        grid=(nseq, n_pages // PAGES_PER_STEP),
        in_specs=[page_spec(i) for i in range(PAGES_PER_STEP)] + [page_spec(i) for i in range(PAGES_PER_STEP)]
        + [pl.BlockSpec((None, seg_len, GROUP_W), lambda s, j, pt: (s, 0, jq)),
           pl.BlockSpec((None, seg_len, GROUP_W), lambda s, j, pt: (s, 0, jk)),
           pl.BlockSpec((None, seg_len, GROUP_W), lambda s, j, pt: (s, 0, jv)),
           pl.BlockSpec((None, rows, TILE), lambda s, j, pt: (s, 0, 0)),
           pl.BlockSpec((None, rows, TILE), lambda s, j, pt: (s, 0, 0)),
           pl.BlockSpec((None, n_pages, SW), lambda s, j, pt: (s, 0, 0)),
           cs((1, GROUP_W)), cs((GROUP_W, GROUP_W))],
        out_specs=pl.BlockSpec((None, seg_len, GROUP_W), lambda s, j, pt: (s, 0, 0)),
        scratch_shapes=[pltpu.VMEM((rows, GROUP_W), bf16), pltpu.VMEM((rows, 1), f32), pltpu.VMEM((rows, 1), f32),
                        pltpu.VMEM((rows, GROUP_W), f32)],
    )
    return pl.pallas_call(
        functools.partial(_foxs_kernel, seg_len=seg_len),
        grid_spec=grid_spec,
        out_shape=jax.ShapeDtypeStruct((nseq, seg_len, GROUP_W), bf16),
        compiler_params=pltpu.CompilerParams(dimension_semantics=("arbitrary", "arbitrary"), vmem_limit_bytes=VMEM_LIMIT),
        name="fox_sample",
    )(page_table, *([cache_k] * PAGES_PER_STEP), *([cache_v] * PAGES_PER_STEP), z3, z3, z3, cn_bc, cn_row, suffix, gm, bd)


def _layer_norm(x, g, b):
    mu = jnp.mean(x, axis=-1, keepdims=True)
    xc = x - mu
    var = jnp.mean(xc * xc, axis=-1, keepdims=True)
    return xc * lax.rsqrt(var + LN_EPS) * g + b


def _outproj_kernel(ya_ref, yb_ref, yc_ref, yd_ref, x_ref, w_ref, g_ref, b_ref, o_ref, *, alpha):
    acc = _dot(ya_ref[...], w_ref[0])
    acc += _dot(yb_ref[...], w_ref[1])
    acc += _dot(yc_ref[...], w_ref[2])
    acc += _dot(yd_ref[...], w_ref[3])
    o_ref[...] = _layer_norm(alpha * x_ref[...] + acc, g_ref[...], b_ref[...])


def _outproj_ln(ys, x2d, w_out4, g, b, alpha):
    n = x2d.shape[0]
    tm = min(512, n)
    yspec = pl.BlockSpec((tm, GROUP_W), lambda i: (i, 0))
    return pl.pallas_call(
        functools.partial(_outproj_kernel, alpha=alpha),
        grid=(n // tm,),
        in_specs=[yspec, yspec, yspec, yspec, pl.BlockSpec((tm, D_MODEL), lambda i: (i, 0)),
                  pl.BlockSpec((4, GROUP_W, D_MODEL), lambda i: (0, 0, 0)),
                  pl.BlockSpec((1, D_MODEL), lambda i: (0, 0)), pl.BlockSpec((1, D_MODEL), lambda i: (0, 0))],
        out_specs=pl.BlockSpec((tm, D_MODEL), lambda i: (i, 0)),
        out_shape=jax.ShapeDtypeStruct((n, D_MODEL), f32),
        compiler_params=pltpu.CompilerParams(dimension_semantics=("arbitrary",), vmem_limit_bytes=VMEM_LIMIT),
        name="outproj_ln",
    )(*ys, x2d, w_out4, g, b)


def _moe_kernel(x_ref, wr_ref, br_ref, w1_ref, w3_ref, w2_ref, g_ref, b_ref, o_ref, xb_scr, gate_scr, acc_scr, *, alpha):
    e = pl.program_id(1)

    @pl.when(e == 0)
    def _():
        xb = x_ref[...].astype(bf16)
        xb_scr[...] = xb
        logits = _dot(xb, wr_ref[...]) + br_ref[...]
        lane = lax.broadcasted_iota(jnp.int32, logits.shape, 1)
        lanef = lane.astype(f32)
        big = jnp.float32(1 << 20)
        glog = jnp.where(lane < MOE_GROUPS, logits, -jnp.inf)
        gmax = jnp.max(glog, axis=-1, keepdims=True)
        g_idx = jnp.min(jnp.where(glog == gmax, lanef, big), axis=-1, keepdims=True).astype(jnp.int32)
        g_w = 1.0 / jnp.sum(jnp.exp(glog - gmax), axis=-1, keepdims=True)
        egrp = lax.shift_right_logical(lane - MOE_GROUPS, 2)
        in_grp = (lane >= MOE_GROUPS) & (lane < MOE_GROUPS + N_EXPERTS) & (egrp == g_idx)
        elog = jnp.where(in_grp, logits, -jnp.inf)
        v1 = jnp.max(elog, axis=-1, keepdims=True)
        i1 = jnp.min(jnp.where(elog == v1, lanef, big), axis=-1, keepdims=True).astype(jnp.int32)
        elog2 = jnp.where(lane == i1, -jnp.inf, elog)
        v2 = jnp.max(elog2, axis=-1, keepdims=True)
        i2 = jnp.min(jnp.where(elog2 == v2, lanef, big), axis=-1, keepdims=True).astype(jnp.int32)
        e2 = jnp.exp(v2 - v1)
        den = 1.0 + e2
        gate_scr[...] = jnp.where(lane == i1, g_w * (1.0 / den), jnp.where(lane == i2, g_w * (e2 / den), 0.0))
        acc_scr[...] = jnp.zeros_like(acc_scr)

    xb = xb_scr[...]
    gates = gate_scr[...]
    lane = lax.broadcasted_iota(jnp.int32, gates.shape, 1)
    ge = jnp.sum(jnp.where(lane == e + MOE_GROUPS, gates, 0.0), axis=-1, keepdims=True)
    a = _dot(xb, w1_ref[...])
    hmid = (a * jax.nn.sigmoid(a)) * _dot(xb, w3_ref[...])
    acc_scr[...] += _dot((hmid * ge).astype(bf16), w2_ref[...])

    @pl.when(e == pl.num_programs(1) - 1)
    def _():
        o_ref[...] = _layer_norm(alpha * x_ref[...] + acc_scr[...], g_ref[...], b_ref[...])


def _moe_ln(x2d, wr, br, w1, w3, w2, g, b, alpha):
    n = x2d.shape[0]
    tm = min(1024, n)
    return pl.pallas_call(
        functools.partial(_moe_kernel, alpha=alpha),
        grid=(n // tm, N_EXPERTS),
        in_specs=[pl.BlockSpec((tm, D_MODEL), lambda i, e: (i, 0)),
                  pl.BlockSpec((D_MODEL, TILE), lambda i, e: (0, 0)),
                  pl.BlockSpec((1, TILE), lambda i, e: (0, 0)),
                  pl.BlockSpec((None, D_MODEL, D_EXPERT), lambda i, e: (e, 0, 0)),
                  pl.BlockSpec((None, D_MODEL, D_EXPERT), lambda i, e: (e, 0, 0)),
                  pl.BlockSpec((None, D_EXPERT, D_MODEL), lambda i, e: (e, 0, 0)),
                  pl.BlockSpec((1, D_MODEL), lambda i, e: (0, 0)), pl.BlockSpec((1, D_MODEL), lambda i, e: (0, 0))],
        out_specs=pl.BlockSpec((tm, D_MODEL), lambda i, e: (i, 0)),
        out_shape=jax.ShapeDtypeStruct((n, D_MODEL), f32),
        scratch_shapes=[pltpu.VMEM((tm, D_MODEL), bf16), pltpu.VMEM((tm, TILE), f32), pltpu.VMEM((tm, D_MODEL), f32)],
        compiler_params=pltpu.CompilerParams(dimension_semantics=("arbitrary", "arbitrary"), vmem_limit_bytes=VMEM_LIMIT),
        name="moe_ln",
    )(x2d, wr, br, w1, w3, w2, g, b)


def _tile_structure(nseg):
    seg_len = TILE // nseg
    r = np.arange(TILE)
    return seg_len, r // seg_len, r % seg_len


def _structure_consts(nseg):
    seg_len, seg, pos = _tile_structure(nseg)
    same = seg[:, None] == seg[None, :]
    maskc = (same & (pos[None, :] <= pos[:, None])).astype(np.float32)
    last = (np.arange(TILE)[None, :] == (seg * seg_len + seg_len - 1)[:, None]).astype(np.float32)
    lane = np.arange(GROUP_W)
    col = np.arange(SW)
    seli = (lane[:, None] == (GL_CI + col[None, :] // TILE)).astype(np.float32)
    self_ = (lane[:, None] == (GL_CF + col[None, :] // TILE)).astype(np.float32)
    lane_t = np.arange(TILE)
    selc = (lane_t[:, None] == (GL_DF + col[None, :] // TILE)).astype(np.float32)
    bd = (lane[:, None] // HEAD_DIM == lane[None, :] // HEAD_DIM).astype(np.float32)
    return dict(maskc=jnp.asarray(maskc), tri=jnp.asarray(maskc, bf16), last=jnp.asarray(last, bf16),
                seli=jnp.asarray(seli, bf16), self_=jnp.asarray(self_, bf16), selc=jnp.asarray(selc, bf16),
                bd=jnp.asarray(bd, bf16), bdf=jnp.asarray(bd))


def _retention_consts(nseg):
    seg_len, seg, pos = _tile_structure(nseg)
    same = jnp.asarray(seg[:, None] == seg[None, :])
    log_g = jnp.log1p(-jnp.exp2(-5.0 - jnp.arange(HEADS, dtype=f32)))
    posf = jnp.asarray(pos, f32)
    diff = posf[:, None] - posf[None, :]
    decay = jnp.where(same[None] & (diff >= 0)[None], jnp.exp(log_g[:, None, None] * jnp.maximum(diff, 0.0)[None]), 0.0)
    dec_all = jnp.transpose(decay, (1, 0, 2)).reshape(TILE, SW)
    rep = lambda a: jnp.repeat(a.T, HEAD_DIM, axis=1)
    qdec = rep(jnp.exp(log_g[:, None] * (posf[None, :] + 1.0)))
    kdec = rep(jnp.exp(log_g[:, None] * (seg_len - 1.0 - posf[None, :])))
    cdec = jnp.repeat(jnp.exp(log_g * seg_len), HEAD_DIM)
    lane = np.arange(GROUP_W)
    bdf = jnp.asarray((lane[:, None] // HEAD_DIM == lane[None, :] // HEAD_DIM).astype(np.float32))
    return dec_all, qdec, kdec, cdec[:, None] * bdf, bdf


def _rope_tables(pos):
    half = HEAD_DIM // 2
    inv = ROPE_BASE ** (-jnp.arange(half, dtype=f32) / half)
    ang = pos.astype(f32)[:, None] * inv[None, :]
    cos, sin = jnp.cos(ang), jnp.sin(ang)
    cos_t = jnp.tile(jnp.concatenate([cos, cos], axis=1), (1, HEADS))
    sin_t = jnp.tile(jnp.concatenate([-sin, sin], axis=1), (1, HEADS))
    return cos_t, sin_t


def _prep_inproj(w_in, b_in):
    scale = HEAD_DIM ** -0.5
    lane = np.arange(GROUP_W)
    swap = (lane // HEAD_DIM) * HEAD_DIM + (lane % HEAD_DIM + HEAD_DIM // 2) % HEAD_DIM

    def blk(a, name, s=1.0, perm=None):
        o = _col_offset(name)
        c = a[..., o:o + GROUP_W]
        if perm is not None:
            c = c[..., perm]
        return c * s if s != 1.0 else c

    def gates(a):
        parts = [a[..., _col_offset(n):_col_offset(n) + HEADS] for n in ('d_f', 'c_i', 'c_f')]
        pad = jnp.zeros(a.shape[:-1] + (GROUP_W - 3 * HEADS,), a.dtype)
        return jnp.concatenate(parts + [pad], axis=-1)

    def build(a):
        return jnp.concatenate([
            blk(a, 'a_u'), blk(a, 'a_v'), blk(a, 'b_q'), blk(a, 'b_q', perm=swap),
            blk(a, 'b_k', scale), blk(a, 'b_k', scale, swap), blk(a, 'b_v'), blk(a, 'b_g'),
            blk(a, 'c_q'), blk(a, 'c_k', scale), blk(a, 'c_v'), blk(a, 'c_o'),
            blk(a, 'd_q', scale), blk(a, 'd_k'), blk(a, 'd_v'), gates(a)], axis=-1)

    return build(w_in).astype(bf16), build(b_in[None, :])


def _prep_mix_a(w_s, b_s, nseg):
    seg_len, seg, pos = _tile_structure(nseg)
    same = seg[:, None] == seg[None, :]
    keep = jnp.asarray(same & (pos[None, :] <= pos[:, None]))
    w_blk = jnp.where(keep[None], w_s[:, pos][:, :, pos], 0.0).astype(bf16)
    bmap = jnp.repeat(b_s[:, pos].T, HEAD_DIM, axis=1)
    return w_blk, bmap


def _block_diag_state(s):
    n = s.shape[0]
    eye = jnp.eye(HEADS, dtype=s.dtype)
    return jnp.einsum('nhde,hg->nhdge', s, eye).reshape(n, GROUP_W, GROUP_W)


def _diag_blocks(s):
    n = s.shape[0]
    s5 = s.reshape(n, HEADS, HEAD_DIM, HEADS, HEAD_DIM)
    return jnp.stack([s5[:, h, :, h, :] for h in range(HEADS)], axis=1)


def _heads(a):
    return a.reshape(a.shape[:-1] + (HEADS, HEAD_DIM))


def _mix_group(z, x2d, lw, sc, rc, cos, sin, nseg, init, fox_fn, alpha):
    bd = sc['bd']
    gm = lw['g_mix']
    y_a, v_a = _mix_a(z, lw['w_blk'][nseg], lw['bmap'][nseg], gm[0], bd)
    r0 = None if init is None else init['r0']
    y_b, r_new = _mix_b(z, cos, sin, rc, gm[1], bd, r0, nseg)
    ci = None if init is None else (init['c0'], init['n0'], init['m0'])
    y_c, c_new, n_new, m_new = _mix_c(z, (sc['maskc'], sc['tri'], sc['last'], sc['seli'], sc['self_'], sc['bdf']),
                                      gm[2], bd, ci, nseg)
    y_d, logf = fox_fn(z, gm[3], bd)
    n2 = x2d.shape[0]
    ys = [y.reshape(n2, GROUP_W) for y in (y_a, y_b, y_c, y_d)]
    h1 = _outproj_ln(ys, x2d, lw['w_out4'], lw['ln1_g'], lw['ln1_b'], alpha)
    return h1, dict(v_a=v_a, r=r_new, c=c_new, n=n_new, m=m_new, logf=logf)


def kernel(x_prompt, x_sample, cache_k, cache_v, cache_logf, page_table, state_ret, state_mlstm_c, state_mlstm_n, state_mlstm_m, w_in, b_in, w_s, b_s, g_mix, w_out, ln1_g, ln1_b, w_rg, b_rg, w_re, b_re, w1, w3, w2, ln2_g, ln2_b):
    depth = w_in.shape[0]
    alpha = (2 * depth) ** 0.25
    bp, tp = x_prompt.shape[:2]
    bs, ts = x_sample.shape[:2]
    n_pages = page_table.shape[1]
    past_len = n_pages * PAGE_SIZE
    nseg_s = TILE // ts
    tiles_s = (bs * ts) // TILE
    n_pool = cache_k.shape[1]

    sc_p, sc_s = _structure_consts(1), _structure_consts(nseg_s)
    rc_p, rc_s = _retention_consts(1), _retention_consts(nseg_s)
    cos_p, sin_p = _rope_tables(jnp.arange(tp, dtype=jnp.int32))
    pos_s = past_len + jnp.arange(ts, dtype=jnp.int32)
    cos_s, sin_s = _rope_tables(jnp.tile(pos_s, nseg_s))

    src = np.arange(PAGE_SIZE * HEADS)
    dst = np.arange(SW)
    same_head = (src[:, None] % HEADS) == (dst[None, :] // TILE)
    mrev = jnp.asarray((same_head & ((src[:, None] // HEADS) > (dst[None, :] % TILE))).astype(np.float32), bf16)
    mtot = jnp.asarray(same_head.astype(np.float32), bf16)
    pg = np.arange(n_pages)
    trev = jnp.asarray((pg[None, :] > pg[:, None]).astype(np.float32), bf16)

    hp = x_prompt.reshape(bp * tp, D_MODEL)
    hs = x_sample.reshape(bs * ts, D_MODEL)
    outs_p, outs_s = [], []
    for l in range(depth):
        w_big, b_big = _prep_inproj(w_in[l], b_in[l])
        lw = dict(
            g_mix=g_mix[l].reshape(4, 1, GROUP_W),
            w_out4=w_out[l].astype(bf16).reshape(4, GROUP_W, D_MODEL),
            ln1_g=ln1_g[l][None, :], ln1_b=ln1_b[l][None, :],
            w_blk={}, bmap={})
        for nseg in (1, nseg_s):
            lw['w_blk'][nseg], lw['bmap'][nseg] = _prep_mix_a(w_s[l], b_s[l], nseg)
        wr = jnp.concatenate([w_rg[l], jnp.transpose(w_re[l], (1, 0, 2)).reshape(D_MODEL, N_EXPERTS),
                              jnp.zeros((D_MODEL, TILE - MOE_GROUPS - N_EXPERTS), f32)], axis=1).astype(bf16)
        br = jnp.concatenate([b_rg[l], b_re[l].reshape(N_EXPERTS), jnp.zeros((TILE - MOE_GROUPS - N_EXPERTS,), f32)])[None, :]
        w1b, w3b, w2b = w1[l].astype(bf16), w3[l].astype(bf16), w2[l].astype(bf16)

        zp = _inproj(hp, w_big, b_big).reshape(bp, tp, Z_W)

        def fox_p(z, gm, bd):
            lf, cum_col, cum_row = _fox_pre(z, sc_p['tri'], True)
            return _fox_prompt(z, cum_col, cum_row, sc_p['selc'], gm, bd), lf

        h1, st = _mix_group(zp, hp, lw, sc_p, rc_p, cos_p, sin_p, 1, None, fox_p, alpha)
        hp = _moe_ln(h1, wr, br, w1b, w3b, w2b, ln2_g[l][None, :], ln2_b[l][None, :], alpha)
        outs_p.append(dict(
            k=_heads(zp[..., ZB['d_k'] * GROUP_W:(ZB['d_k'] + 1) * GROUP_W]),
            v=_heads(zp[..., ZB['d_v'] * GROUP_W:(ZB['d_v'] + 1) * GROUP_W]),
            logf=st['logf'][..., GL_DF:GL_DF + HEADS],
            r=_diag_blocks(st['r']), c=_diag_blocks(st['c']),
            n=st['n'][:, :, 0].reshape(bp, HEADS, HEAD_DIM),
            m=st['m'][:, 0, :].reshape(bp, HEADS, TILE)[:, :, 0]))

        zs = _inproj(hs, w_big, b_big)
        zs_t = zs.reshape(tiles_s, TILE, Z_W)
        init = dict(
            r0=_block_diag_state(state_ret[l].astype(f32)),
            c0=_block_diag_state(state_mlstm_c[l].astype(f32)),
            n0=jnp.broadcast_to(state_mlstm_n[l].astype(f32).reshape(bs, GROUP_W, 1), (bs, GROUP_W, TILE)),
            m0=jnp.broadcast_to(state_mlstm_m[l].astype(f32)[:, None, :, None], (bs, ts, HEADS, TILE)).reshape(tiles_s, TILE, SW))
        ck = cache_k[l].reshape(n_pool, PAGE_SIZE, GROUP_W)
        cv = cache_v[l].reshape(n_pool, PAGE_SIZE, GROUP_W)
        suffix = _fox_suffix(cache_logf[l].reshape(n_pool, PAGE_SIZE * HEADS), page_table, mrev, mtot, trev)

        def fox_s(z, gm, bd):
            z3 = z.reshape(bs, ts, Z_W)
            lf_t, cum_t, _ = _fox_pre(z, sc_s['tri'], False)
            lf = lf_t[..., GL_DF:GL_DF + HEADS].reshape(bs, ts, HEADS)
            cn_ht = jnp.transpose(cum_t[..., GL_DF:GL_DF + HEADS].reshape(bs, ts, HEADS), (0, 2, 1))
            cn_bc = jnp.broadcast_to(cn_ht.reshape(bs, HEADS * ts, 1), (bs, HEADS * ts, TILE))
            cn_row = jnp.broadcast_to(cn_ht[:, :, None, :], (bs, HEADS, ts, ts)).reshape(bs, HEADS * ts, ts)
            cn_row = jnp.pad(cn_row, ((0, 0), (0, 0), (0, TILE - ts)))
            y = _fox_sample(z3, ck, cv, page_table, cn_bc, cn_row, suffix, gm, bd)
            return y.reshape(tiles_s, TILE, GROUP_W), lf

        h1, st = _mix_group(zs_t, hs, lw, sc_s, rc_s, cos_s, sin_s, nseg_s, init, fox_s, alpha)
        hs = _moe_ln(h1, wr, br, w1b, w3b, w2b, ln2_g[l][None, :], ln2_b[l][None, :], alpha)
        zs3 = zs.reshape(bs, ts, Z_W)
        outs_s.append(dict(
            k=_heads(zs3[..., ZB['d_k'] * GROUP_W:(ZB['d_k'] + 1) * GROUP_W]),
            v=_heads(zs3[..., ZB['d_v'] * GROUP_W:(ZB['d_v'] + 1) * GROUP_W]),
            logf=st['logf'],
            r=_diag_blocks(st['r']), c=_diag_blocks(st['c']),
            n=st['n'][:, :, 0].reshape(bs, HEADS, HEAD_DIM),
            m=st['m'].reshape(bs, ts, HEADS, TILE)[:, 0, :, 0],
            v_a=_heads(st['v_a'].reshape(bs, ts, GROUP_W))))

    dt = x_prompt.dtype
    stk = lambda outs, key: jnp.stack([o[key] for o in outs]).astype(dt)
    return (hp.reshape(bp, tp, D_MODEL), hs.reshape(bs, ts, D_MODEL),
            stk(outs_p, 'k'), stk(outs_p, 'v'), stk(outs_p, 'logf'),
            stk(outs_p, 'r'), stk(outs_p, 'c'), stk(outs_p, 'n'), stk(outs_p, 'm'),
            stk(outs_s, 'k'), stk(outs_s, 'v'), stk(outs_s, 'logf'),
            stk(outs_s, 'r'), stk(outs_s, 'c'), stk(outs_s, 'n'), stk(outs_s, 'm'),
            stk(outs_s, 'v_a'))
```
